```python
import math
import jax, jax.numpy as jnp
from jax import lax
import numpy as np

D_MODEL = 1024
BATCH = 8
SEQ = 2048
DEPTH = 4
DEC_BATCH = 128
DEC_SEQ = 4
PAST_LEN = 16384
PAGE_SIZE = 128

N_EVEN = (DEPTH + 1) // 2
N_ODD = DEPTH // 2
D_A = D_MODEL // 2
CONV_W = 31
D_B = D_MODEL // 2
POOL_WINDOWS = (2, 4, 8, 16)
N_POOL_GROUPS = len(POOL_WINDOWS)
POOL_G = D_B // N_POOL_GROUPS
MAX_WIN = max(POOL_WINDOWS)
D_C = D_MODEL
H_C = 8
DH_C = D_C // H_C
CHUNK = 128
D_FF = 4 * D_MODEL
N_MOD = 6
EPS = 1e-6

kernel_name = "hybrid_conv_pool_gmlp_decoder_step"


def rmsnorm(x, g):
    xf = x.astype(jnp.float32)
    y = xf * lax.rsqrt(jnp.mean(xf * xf, axis=-1, keepdims=True) + EPS)
    return (y * g.astype(jnp.float32)).astype(x.dtype)


def layernorm(x, g, b):
    xf = x.astype(jnp.float32)
    mu = jnp.mean(xf, axis=-1, keepdims=True)
    xc = xf - mu
    var = jnp.mean(xc * xc, axis=-1, keepdims=True)
    y = xc * lax.rsqrt(var + EPS) * g.astype(jnp.float32) + b.astype(jnp.float32)
    return y.astype(x.dtype)


def adaln(c, w_ada, b_ada):
    mod = jax.nn.silu(c) @ w_ada + b_ada
    mod = mod.reshape(c.shape[0], N_MOD, D_MODEL)[:, :, None, :]
    return [mod[:, i] for i in range(N_MOD)]


def even_mixer(h, conv_hist, pool_hist, pos0, w_in, w_dw, b_dw, g_cln, b_cln, w_pool, s_pool, w_out):
    T = h.shape[1]
    z = h @ w_in
    a_val, a_gate, p_in = z[..., :D_A], z[..., D_A:2 * D_A], z[..., 2 * D_A:]
    a = a_val * jax.nn.sigmoid(a_gate)
    a_ext = jnp.concatenate([conv_hist.astype(a.dtype), a], axis=1)
    conv = lax.conv_general_dilated(
        a_ext, w_dw[:, None, :], window_strides=(1,), padding='VALID',
        dimension_numbers=('NWC', 'WIO', 'NWC'), feature_group_count=D_A) + b_dw
    a_out = jax.nn.silu(layernorm(conv, g_cln, b_cln))
    P = MAX_WIN - 1
    p_ext = jnp.concatenate([pool_hist.astype(p_in.dtype), p_in], axis=1)
    cs = jnp.cumsum(p_ext.astype(jnp.float32), axis=1)
    cs = jnp.concatenate([jnp.zeros_like(cs[:, :1]), cs], axis=1)
    pos = pos0 + jnp.arange(T) + 1
    diffs = []
    for gi, w in enumerate(POOL_WINDOWS):
        sl = slice(gi * POOL_G, (gi + 1) * POOL_G)
        win_sum = cs[:, P + 1:P + 1 + T, sl] - cs[:, P + 1 - w:P + 1 - w + T, sl]
        cnt = jnp.minimum(w, pos).astype(jnp.float32)[None, :, None]
        diffs.append((win_sum / cnt).astype(p_in.dtype) - p_in[..., sl])
    d = jnp.stack(diffs, axis=2)
    b_out = jnp.einsum('btgi,gio->btgo', d, w_pool).reshape(h.shape[0], T, D_B) * s_pool
    out = jnp.concatenate([a_out, b_out], axis=-1) @ w_out
    return out, a_ext[:, -(CONV_W - 1):], p_ext[:, -(MAX_WIN - 1):]


def odd_mixer(h, w_in, g_v, b_v, w_s, b_s, w_out):
    B, T, _ = h.shape
    z = h @ w_in
    u, v = z[..., :D_C], z[..., D_C:]
    v = layernorm(v, g_v, b_v)
    L = min(T, CHUNK)
    n = T // L
    vc = v.reshape(B, n, L, H_C, DH_C)
    mask = jnp.tril(jnp.ones((CHUNK, CHUNK), dtype=w_s.dtype))
    ws = (w_s * mask)[:, :L, :L]
    s = jnp.einsum('hts,bnshd->bnthd', ws, vc) + b_s[:, :L].T[None, None, :, :, None]
    y = u * s.reshape(B, T, D_C)
    return y @ w_out, v


def trunk(x, c, conv_hist, pool_hist, pos0,
          w_ada, b_ada, g_mix, g_ffn, w_in_ab, w_dw, b_dw, g_conv_ln, b_conv_ln,
          w_pool, s_pool, w_out_ab, w_in_c, g_v_ln, b_v_ln, w_spatial, b_spatial,
          w_out_c, w_ffn1, w_ffn2, g_final):
    new_conv, new_pool, new_v = [], [], []
    for l in range(DEPTH):
        sh1, sc1, gt1, sh2, sc2, gt2 = adaln(c, w_ada[l], b_ada[l])
        h = rmsnorm(x, g_mix[l]) * (1 + sc1) + sh1
        if l % 2 == 0:
            i = l // 2
            out, ch, ph = even_mixer(h, conv_hist[i], pool_hist[i], pos0, w_in_ab[i], w_dw[i], b_dw[i],
                                     g_conv_ln[i], b_conv_ln[i], w_pool[i], s_pool[i], w_out_ab[i])
            new_conv.append(ch)
            new_pool.append(ph)
        else:
            j = l // 2
            out, vrows = odd_mixer(h, w_in_c[j], g_v_ln[j], b_v_ln[j], w_spatial[j], b_spatial[j], w_out_c[j])
            new_v.append(vrows)
        x = x + gt1 * out
        h = rmsnorm(x, g_ffn[l]) * (1 + sc2) + sh2
        f = jnp.square(jax.nn.relu(h @ w_ffn1[l])) @ w_ffn2[l]
        x = x + gt2 * f
    return rmsnorm(x, g_final), jnp.stack(new_conv), jnp.stack(new_pool), jnp.stack(new_v)


def setup_inputs(seed: int = 0) -> dict:
    key = jax.random.key(seed)
    ks = iter(jax.random.split(key, 40))
    f32 = jnp.float32
    nrm = lambda shape, s: jax.random.normal(next(ks), shape, f32) * s
    gain = lambda shape: 1.0 + nrm(shape, 0.1)
    return {
        "x_prompt": nrm((BATCH, SEQ, D_MODEL), 1.0),
        "x_sample": nrm((DEC_BATCH, DEC_SEQ, D_MODEL), 1.0),
        "c_prompt": nrm((BATCH, D_MODEL), 1.0),
        "c_sample": nrm((DEC_BATCH, D_MODEL), 1.0),
        "state_conv": nrm((N_EVEN, DEC_BATCH, CONV_W - 1, D_A), 0.5),
        "state_pool": nrm((N_EVEN, DEC_BATCH, MAX_WIN - 1, D_B), 1.0),
        "w_ada": nrm((DEPTH, D_MODEL, N_MOD * D_MODEL), 0.5 * D_MODEL ** -0.5),
        "b_ada": nrm((DEPTH, N_MOD * D_MODEL), 0.1),
        "g_mix": gain((DEPTH, D_MODEL)),
        "g_ffn": gain((DEPTH, D_MODEL)),
        "w_in_ab": nrm((N_EVEN, D_MODEL, 2 * D_A + D_B), D_MODEL ** -0.5),
        "w_dw": nrm((N_EVEN, CONV_W, D_A), CONV_W ** -0.5),
        "b_dw": nrm((N_EVEN, D_A), 0.02),
        "g_conv_ln": gain((N_EVEN, D_A)),
        "b_conv_ln": nrm((N_EVEN, D_A), 0.02),
        "w_pool": nrm((N_EVEN, N_POOL_GROUPS, POOL_G, POOL_G), POOL_G ** -0.5),
        "s_pool": gain((N_EVEN, D_B)),
        "w_out_ab": nrm((N_EVEN, D_A + D_B, D_MODEL), (D_A + D_B) ** -0.5),
        "w_in_c": nrm((N_ODD, D_MODEL, 2 * D_C), D_MODEL ** -0.5),
        "g_v_ln": gain((N_ODD, D_C)),
        "b_v_ln": nrm((N_ODD, D_C), 0.02),
        "w_spatial": nrm((N_ODD, H_C, CHUNK, CHUNK), CHUNK ** -0.5),
        "b_spatial": gain((N_ODD, H_C, CHUNK)),
        "w_out_c": nrm((N_ODD, D_C, D_MODEL), D_C ** -0.5),
        "w_ffn1": nrm((DEPTH, D_MODEL, D_FF), D_MODEL ** -0.5),
        "w_ffn2": nrm((DEPTH, D_FF, D_MODEL), D_FF ** -0.5),
        "g_final": gain((D_MODEL,)),
    }


def reference(x_prompt, x_sample, c_prompt, c_sample, state_conv, state_pool,
              w_ada, b_ada, g_mix, g_ffn, w_in_ab, w_dw, b_dw, g_conv_ln, b_conv_ln,
              w_pool, s_pool, w_out_ab, w_in_c, g_v_ln, b_v_ln, w_spatial, b_spatial,
              w_out_c, w_ffn1, w_ffn2, g_final):
    weights = (w_ada, b_ada, g_mix, g_ffn, w_in_ab, w_dw, b_dw, g_conv_ln, b_conv_ln,
               w_pool, s_pool, w_out_ab, w_in_c, g_v_ln, b_v_ln, w_spatial, b_spatial,
               w_out_c, w_ffn1, w_ffn2, g_final)
    conv0 = jnp.zeros((N_EVEN, x_prompt.shape[0], CONV_W - 1, D_A), x_prompt.dtype)
    pool0 = jnp.zeros((N_EVEN, x_prompt.shape[0], MAX_WIN - 1, D_B), x_prompt.dtype)
    y_prompt, new_conv_prompt, new_pool_prompt, _ = trunk(
        x_prompt, c_prompt, conv0, pool0, 0, *weights)
    y_sample, new_conv_sample, new_pool_sample, new_v_sample = trunk(
        x_sample, c_sample, state_conv, state_pool, PAST_LEN, *weights)
    return (y_prompt, y_sample, new_conv_prompt, new_pool_prompt,
            new_conv_sample, new_pool_sample, new_v_sample)
```

```python
import functools

import jax
import jax.numpy as jnp
from jax import lax
from jax.experimental import pallas as pl
from jax.experimental.pallas import tpu as pltpu

D_MODEL = 1024
DEPTH = 4
PAST_LEN = 16384
D_A = D_MODEL // 2
CONV_W = 31
D_B = D_MODEL // 2
POOL_WINDOWS = (2, 4, 8, 16)
POOL_G = D_B // len(POOL_WINDOWS)
MAX_WIN = max(POOL_WINDOWS)
D_C = D_MODEL
H_C = 8
DH_C = D_C // H_C
CHUNK = 128
D_FF = 4 * D_MODEL
N_MOD = 6
EPS = 1e-6

TM = 512
CONV_ROWS = 64
FF_CHUNK = 1024
A_HIST = 32
P_HIST = 16
ADA_NB = 1536
VMEM_LIMIT = 56 * 1024 * 1024

_F32 = jnp.float32
_BF16 = jnp.bfloat16


def _dot(a, b):
    return jnp.dot(a, b, preferred_element_type=_F32)


def _split_mod(mod):
    return [mod[:, i * D_MODEL:(i + 1) * D_MODEL] for i in range(N_MOD)]


def _rms(x, g):
    return x * lax.rsqrt(jnp.mean(x * x, axis=-1, keepdims=True) + EPS) * g


def _layernorm(x, g, b):
    mu = jnp.mean(x, axis=-1, keepdims=True)
    xc = x - mu
    var = jnp.mean(xc * xc, axis=-1, keepdims=True)
    return xc * lax.rsqrt(var + EPS) * g + b


def _silu(x):
    return x * jax.nn.sigmoid(x)


def _ffn(x, g, sc, sh, gt, w1_ref, w2_ref):
    h = (_rms(x, g) * (1.0 + sc) + sh).astype(_BF16)
    acc = None
    for c in range(D_FF // FF_CHUNK):
        cols = slice(c * FF_CHUNK, (c + 1) * FF_CHUNK)
        hid = jnp.maximum(_dot(h, w1_ref[:, cols]), 0.0)
        part = _dot((hid * hid).astype(_BF16), w2_ref[cols, :])
        acc = part if acc is None else acc + part
    return x + gt * acc


def _adaln_kernel(c_ref, w_ref, b_ref, o_ref):
    s = _silu(c_ref[...]).astype(_BF16)
    o_ref[...] = _dot(s, w_ref[...].astype(_BF16)) + b_ref[...]


def _adaln(c_all, w_ada, b_ada):
    n_rows = c_all.shape[0]
    n_out = N_MOD * D_MODEL
    return pl.pallas_call(
        _adaln_kernel,
        grid=(DEPTH, n_out // ADA_NB),
        in_specs=[
            pl.BlockSpec((n_rows, D_MODEL), lambda l, n: (0, 0)),
            pl.BlockSpec((None, D_MODEL, ADA_NB), lambda l, n: (l, 0, n)),
            pl.BlockSpec((None, 1, ADA_NB), lambda l, n: (l, 0, n)),
        ],
        out_specs=pl.BlockSpec((None, n_rows, ADA_NB), lambda l, n: (l, 0, n)),
        out_shape=jax.ShapeDtypeStruct((DEPTH, n_rows, n_out), _F32),
        compiler_params=pltpu.CompilerParams(
            dimension_semantics=("arbitrary", "arbitrary"), vmem_limit_bytes=VMEM_LIMIT),
        name="adaln",
    )(c_all, w_ada, b_ada.reshape(DEPTH, 1, n_out))


def _even_prompt_kernel(x_ref, mod_ref, gmix_ref, gffn_ref, gfin_ref, win_ref, wdw_ref, bdw_ref,
                        gcln_ref, bcln_ref, wpool_ref, spool_ref, wout_ref, w1_ref, w2_ref,
                        xo_ref, nconv_ref, npool_ref, aext_ref, pext_ref, cat_ref, *, final):
    t = pl.program_id(1)
    n_t = pl.num_programs(1)

    @pl.when(t == 0)
    def _():
        aext_ref[0:A_HIST, :] = jnp.zeros((A_HIST, D_A), _F32)
        pext_ref[0:P_HIST, :] = jnp.zeros((P_HIST, D_B), _F32)

    x = x_ref[...]
    sh1, sc1, gt1, sh2, sc2, gt2 = _split_mod(mod_ref[...])
    h = (_rms(x, gmix_ref[...]) * (1.0 + sc1) + sh1).astype(_BF16)
    z = _dot(h, win_ref[...])
    aext_ref[A_HIST:A_HIST + TM, :] = z[:, :D_A] * jax.nn.sigmoid(z[:, D_A:2 * D_A])
    pext_ref[P_HIST:P_HIST + TM, :] = z[:, 2 * D_A:]

    first = A_HIST - (CONV_W - 1)
    for r in range(TM // CONV_ROWS):
        acc = jnp.broadcast_to(bdw_ref[...], (CONV_ROWS, D_A))
        for k in range(CONV_W):
            rows = pl.ds(r * CONV_ROWS + first + k, CONV_ROWS)
            acc = acc + aext_ref[rows, :] * wdw_ref[k:k + 1, :]
        a_out = _silu(_layernorm(acc, gcln_ref[...], bcln_ref[...]))
        cat_ref[r * CONV_ROWS:(r + 1) * CONV_ROWS, 0:D_A] = a_out.astype(_BF16)

    pos = t * TM + lax.broadcasted_iota(jnp.int32, (TM, POOL_G), 0) + 1
    for gi, w in enumerate(POOL_WINDOWS):
        lanes = slice(gi * POOL_G, (gi + 1) * POOL_G)
        win = pext_ref[P_HIST:P_HIST + TM, lanes]
        p_g = win
        for j in range(1, w):
            win = win + pext_ref[P_HIST - j:P_HIST - j + TM, lanes]
        cnt = jnp.minimum(w, pos).astype(_F32)
        d = (win / cnt - p_g).astype(_BF16)
        b_out = _dot(d, wpool_ref[gi]) * spool_ref[:, lanes]
        cat_ref[:, D_A + gi * POOL_G:D_A + (gi + 1) * POOL_G] = b_out.astype(_BF16)

    @pl.when(t == n_t - 1)
    def _():
        nconv_ref[...] = aext_ref[A_HIST + TM - (CONV_W - 1):A_HIST + TM, :]
        npool_ref[...] = pext_ref[P_HIST + TM - (MAX_WIN - 1):P_HIST + TM, :]

    aext_ref[0:A_HIST, :] = aext_ref[TM:TM + A_HIST, :]
    pext_ref[0:P_HIST, :] = pext_ref[TM:TM + P_HIST, :]

    x = x + gt1 * _dot(cat_ref[...], wout_ref[...])
    x = _ffn(x, gffn_ref[...], sc2, sh2, gt2, w1_ref, w2_ref)
    if final:
        x = _rms(x, gfin_ref[...])
    xo_ref[...] = x


def _odd_prompt_kernel(x_ref, mod_ref, gmix_ref, gffn_ref, gfin_ref, win_ref, gv_ref, bv_ref,
                       ws_ref, bs_ref, wout_ref, w1_ref, w2_ref, xo_ref, y_ref, *, final):
    x = x_ref[...]
    sh1, sc1, gt1, sh2, sc2, gt2 = _split_mod(mod_ref[...])
    h = (_rms(x, gmix_ref[...]) * (1.0 + sc1) + sh1).astype(_BF16)
    z = _dot(h, win_ref[...])
    u = z[:, :D_C]
    v = _layernorm(z[:, D_C:], gv_ref[...], bv_ref[...]).astype(_BF16)

    row = lax.broadcasted_iota(jnp.int32, (CHUNK, CHUNK), 0)
    col = lax.broadcasted_iota(jnp.int32, (CHUNK, CHUNK), 1)
    for hh in range(H_C):
        lanes = slice(hh * DH_C, (hh + 1) * DH_C)
        ws = jnp.where(row >= col, ws_ref[hh], 0.0).astype(_BF16)
        for c in range(TM // CHUNK):
            rows = slice(c * CHUNK, (c + 1) * CHUNK)
            s = _dot(ws, v[rows, lanes]) + bs_ref[:, lanes]
            y_ref[rows, lanes] = (u[rows, lanes] * s).astype(_BF16)

    x = x + gt1 * _dot(y_ref[...], wout_ref[...])
    x = _ffn(x, gffn_ref[...], sc2, sh2, gt2, w1_ref, w2_ref)
    if final:
        x = _rms(x, gfin_ref[...])
    xo_ref[...] = x


def _vmem_spec():
    return pl.BlockSpec(memory_space=pltpu.VMEM)


def _prompt_layer(x, mod_l, small, weights, *, even, final):
    batch, seq, _ = x.shape
    n_t = seq // TM
    x_spec = pl.BlockSpec((None, TM, D_MODEL), lambda b, t: (b, t, 0))
    mod_spec = pl.BlockSpec((None, 1, N_MOD * D_MODEL), lambda b, t: (b, 0, 0))
    in_specs = [x_spec, mod_spec] + [_vmem_spec() for _ in range(len(small) + len(weights))]
    params = pltpu.CompilerParams(
        dimension_semantics=("arbitrary", "arbitrary"), vmem_limit_bytes=VMEM_LIMIT)
    if even:
        return pl.pallas_call(
            functools.partial(_even_prompt_kernel, final=final),
            grid=(batch, n_t),
            in_specs=in_specs,
            out_specs=[
                x_spec,
                pl.BlockSpec((None, CONV_W - 1, D_A), lambda b, t: (b, 0, 0)),
                pl.BlockSpec((None, MAX_WIN - 1, D_B), lambda b, t: (b, 0, 0)),
            ],
            out_shape=[
                jax.ShapeDtypeStruct(x.shape, _F32),
                jax.ShapeDtypeStruct((batch, CONV_W - 1, D_A), _F32),
                jax.ShapeDtypeStruct((batch, MAX_WIN - 1, D_B), _F32),
            ],
            scratch_shapes=[
                pltpu.VMEM((A_HIST + TM, D_A), _F32),
                pltpu.VMEM((P_HIST + TM, D_B), _F32),
                pltpu.VMEM((TM, D_A + D_B), _BF16),
            ],
            compiler_params=params,
            name="prompt_even",
        )(x, mod_l, *small, *weights)
    return pl.pallas_call(
        functools.partial(_odd_prompt_kernel, final=final),
        grid=(batch, n_t),
        in_specs=in_specs,
        out_specs=x_spec,
        out_shape=jax.ShapeDtypeStruct(x.shape, _F32),
        scratch_shapes=[pltpu.VMEM((TM, D_C), _BF16)],
        compiler_params=params,
        name="prompt_odd",
    )(x, mod_l, *small, *weights)


def _tile_rows(m, reps):
    return jnp.concatenate([m] * reps, axis=0)


def _even_sample_kernel(x_ref, mod_ref, gmix_ref, gffn_ref, gfin_ref, chist_ref, phist_ref, win_ref,
                        wdw_ref, bdw_ref, gcln_ref, bcln_ref, wpool_ref, spool_ref, wout_ref,
                        w1_ref, w2_ref, xo_ref, anew_ref, pnew_ref, cat_ref, *, n_pos, n_seq, final):
    x = x_ref[...]
    sh1, sc1, gt1, sh2, sc2, gt2 = [_tile_rows(m, n_pos) for m in _split_mod(mod_ref[...])]
    h = (_rms(x, gmix_ref[...]) * (1.0 + sc1) + sh1).astype(_BF16)
    z = _dot(h, win_ref[...])
    anew_ref[...] = z[:, :D_A] * jax.nn.sigmoid(z[:, D_A:2 * D_A])
    pnew_ref[...] = z[:, 2 * D_A:]

    def a_slab(j):
        if j < CONV_W - 1:
            return chist_ref[j]
        return anew_ref[(j - (CONV_W - 1)) * n_seq:(j - (CONV_W - 2)) * n_seq, :]

    def p_slab(j, lanes):
        if j < MAX_WIN - 1:
            return phist_ref[j, :, lanes]
        return pnew_ref[(j - (MAX_WIN - 1)) * n_seq:(j - (MAX_WIN - 2)) * n_seq, lanes]

    for t in range(n_pos):
        rows = slice(t * n_seq, (t + 1) * n_seq)
        acc = jnp.broadcast_to(bdw_ref[...], (n_seq, D_A))
        for k in range(CONV_W):
            acc = acc + a_slab(t + k) * wdw_ref[k:k + 1, :]
        a_out = _silu(_layernorm(acc, gcln_ref[...], bcln_ref[...]))
        cat_ref[rows, 0:D_A] = a_out.astype(_BF16)
        for gi, w in enumerate(POOL_WINDOWS):
            lanes = slice(gi * POOL_G, (gi + 1) * POOL_G)
            p_g = p_slab(MAX_WIN - 1 + t, lanes)
            win = p_g
            for j in range(1, w):
                win = win + p_slab(MAX_WIN - 1 + t - j, lanes)
            cnt = float(min(w, PAST_LEN + t + 1))
            d = (win / cnt - p_g).astype(_BF16)
            b_out = _dot(d, wpool_ref[gi]) * spool_ref[:, lanes]
            cat_ref[rows, D_A + gi * POOL_G:D_A + (gi + 1) * POOL_G] = b_out.astype(_BF16)

    x = x + gt1 * _dot(cat_ref[...], wout_ref[...])
    x = _ffn(x, gffn_ref[...], sc2, sh2, gt2, w1_ref, w2_ref)
    if final:
        x = _rms(x, gfin_ref[...])
    xo_ref[...] = x


def _odd_sample_kernel(x_ref, mod_ref, gmix_ref, gffn_ref, gfin_ref, win_ref, gv_ref, bv_ref,
                       wsv_ref, bs_ref, wout_ref, w1_ref, w2_ref, xo_ref, v_ref, y_ref,
                       *, n_pos, n_seq, final):
    x = x_ref[...]
    sh1, sc1, gt1, sh2, sc2, gt2 = [_tile_rows(m, n_pos) for m in _split_mod(mod_ref[...])]
    h = (_rms(x, gmix_ref[...]) * (1.0 + sc1) + sh1).astype(_BF16)
    z = _dot(h, win_ref[...])
    u = z[:, :D_C]
    v_ref[...] = _layernorm(z[:, D_C:], gv_ref[...], bv_ref[...])
    for t in range(n_pos):
        rows = slice(t * n_seq, (t + 1) * n_seq)
        s = jnp.broadcast_to(bs_ref[t:t + 1, :], (n_seq, D_C))
        for src in range(t + 1):
            s = s + wsv_ref[t * n_pos + src:t * n_pos + src + 1, :] * v_ref[src * n_seq:(src + 1) * n_seq, :]
        y_ref[rows, :] = (u[rows, :] * s).astype(_BF16)

    x = x + gt1 * _dot(y_ref[...], wout_ref[...])
    x = _ffn(x, gffn_ref[...], sc2, sh2, gt2, w1_ref, w2_ref)
    if final:
        x = _rms(x, gfin_ref[...])
    xo_ref[...] = x


def _sample_layer(x, mod_l, small, weights, *, n_pos, n_seq, even, final):
    n_in = 2 + len(small) + len(weights)
    params = pltpu.CompilerParams(vmem_limit_bytes=VMEM_LIMIT)
    rows = n_pos * n_seq
    if even:
        return pl.pallas_call(
            functools.partial(_even_sample_kernel, n_pos=n_pos, n_seq=n_seq, final=final),
            in_specs=[_vmem_spec() for _ in range(n_in)],
            out_specs=[_vmem_spec() for _ in range(3)],
            out_shape=[
                jax.ShapeDtypeStruct((rows, D_MODEL), _F32),
                jax.ShapeDtypeStruct((rows, D_A), _F32),
                jax.ShapeDtypeStruct((rows, D_B), _F32),
            ],
            scratch_shapes=[pltpu.VMEM((rows, D_A + D_B), _BF16)],
            compiler_params=params,
            name="sample_even",
        )(x, mod_l, *small, *weights)
    return pl.pallas_call(
        functools.partial(_odd_sample_kernel, n_pos=n_pos, n_seq=n_seq, final=final),
        in_specs=[_vmem_spec() for _ in range(n_in)],
        out_specs=[_vmem_spec() for _ in range(2)],
        out_shape=[
            jax.ShapeDtypeStruct((rows, D_MODEL), _F32),
            jax.ShapeDtypeStruct((rows, D_C), _F32),
        ],
        scratch_shapes=[pltpu.VMEM((rows, D_C), _BF16)],
        compiler_params=params,
        name="sample_odd",
    )(x, mod_l, *small, *weights)


def kernel(x_prompt, x_sample, c_prompt, c_sample, state_conv, state_pool, w_ada, b_ada, g_mix, g_ffn,
           w_in_ab, w_dw, b_dw, g_conv_ln, b_conv_ln, w_pool, s_pool, w_out_ab, w_in_c, g_v_ln, b_v_ln,
           w_spatial, b_spatial, w_out_c, w_ffn1, w_ffn2, g_final):
    n_batch = x_prompt.shape[0]
    n_seq, n_pos, _ = x_sample.shape
    assert x_prompt.shape[1] % TM == 0 and TM % CHUNK == 0 and TM % CONV_ROWS == 0
    assert n_pos <= CHUNK and n_seq % 8 == 0

    row = lambda a: a.reshape(1, -1)
    bf = lambda a: a.astype(_BF16)

    mod = _adaln(jnp.concatenate([c_prompt, c_sample], axis=0), w_ada, b_ada)
    mod_p = mod[:, :n_batch].reshape(DEPTH, n_batch, 1, N_MOD * D_MODEL)
    mod_s = mod[:, n_batch:]

    xp = x_prompt
    xs = x_sample.transpose(1, 0, 2).reshape(n_pos * n_seq, D_MODEL)
    conv_hist = state_conv.transpose(0, 2, 1, 3)
    pool_hist = state_pool.transpose(0, 2, 1, 3)

    conv_p, pool_p, conv_s, pool_s, v_s = [], [], [], [], []
    for l in range(DEPTH):
        final = l == DEPTH - 1
        common = [row(g_mix[l]), row(g_ffn[l]), row(g_final)]
        ffn_w = [bf(w_ffn1[l]), bf(w_ffn2[l])]
        if l % 2 == 0:
            i = l // 2
            small = common
            weights = [bf(w_in_ab[i]), w_dw[i], row(b_dw[i]), row(g_conv_ln[i]), row(b_conv_ln[i]),
                       bf(w_pool[i]), row(s_pool[i]), bf(w_out_ab[i])] + ffn_w
            xp, cp, pp = _prompt_layer(xp, mod_p[l], small, weights, even=True, final=final)
            conv_p.append(cp)
            pool_p.append(pp)
            xs, a_new, p_new = _sample_layer(
                xs, mod_s[l], small + [conv_hist[i], pool_hist[i]], weights,
                n_pos=n_pos, n_seq=n_seq, even=True, final=final)
            unflat = lambda a: a.reshape(n_pos, n_seq, -1).transpose(1, 0, 2)
            conv_s.append(jnp.concatenate([state_conv[i], unflat(a_new)], axis=1)[:, -(CONV_W - 1):])
            pool_s.append(jnp.concatenate([state_pool[i], unflat(p_new)], axis=1)[:, -(MAX_WIN - 1):])
        else:
            j = l // 2
            bias = jnp.repeat(b_spatial[j].T, DH_C, axis=1)
            head = [bf(w_in_c[j]), row(g_v_ln[j]), row(b_v_ln[j])]
            tail = [bias, bf(w_out_c[j])] + ffn_w
            xp = _prompt_layer(xp, mod_p[l], common, head + [w_spatial[j]] + tail, even=False, final=final)
            ws_small = w_spatial[j][:, :n_pos, :n_pos].transpose(1, 2, 0).reshape(n_pos * n_pos, H_C)
            ws_vec = jnp.repeat(ws_small, DH_C, axis=1)
            xs, v_new = _sample_layer(xs, mod_s[l], common, head + [ws_vec] + tail,
                                      n_pos=n_pos, n_seq=n_seq, even=False, final=final)
            v_s.append(v_new.reshape(n_pos, n_seq, D_C).transpose(1, 0, 2))

    y_sample = xs.reshape(n_pos, n_seq, D_MODEL).transpose(1, 0, 2)
    return (xp, y_sample, jnp.stack(conv_p), jnp.stack(pool_p),
            jnp.stack(conv_s), jnp.stack(pool_s), jnp.stack(v_s))
```

```python
import functools

import jax
import jax.numpy as jnp
from jax import lax
from jax.experimental import pallas as pl
from jax.experimental.pallas import tpu as pltpu

D_MODEL = 1024
DEPTH = 4
PAST_LEN = 16384
D_A = D_MODEL // 2
CONV_W = 31
D_B = D_MODEL // 2
POOL_WINDOWS = (2, 4, 8, 16)
POOL_G = D_B // len(POOL_WINDOWS)
MAX_WIN = max(POOL_WINDOWS)
D_C = D_MODEL
H_C = 8
DH_C = D_C // H_C
CHUNK = 128
D_FF = 4 * D_MODEL
N_MOD = 6
EPS = 1e-6

SUBLANES = 8
TM = 512
SUB = 256
CONV_ROWS = 128
FF_CHUNK = 1024
A_HIST = 32
P_HIST = 16
ADA_NB = 1536
VMEM_LIMIT = 56 * 1024 * 1024

_F32 = jnp.float32
_BF16 = jnp.bfloat16


def _dot(a, b):
    return jnp.dot(a, b, preferred_element_type=_F32)


def _split_mod(mod):
    return [mod[:, i * D_MODEL:(i + 1) * D_MODEL] for i in range(N_MOD)]


def _rms(x, g):
    return x * lax.rsqrt(jnp.mean(x * x, axis=-1, keepdims=True) + EPS) * g


def _layernorm(x, g, b):
    mu = jnp.mean(x, axis=-1, keepdims=True)
    xc = x - mu
    var = jnp.mean(xc * xc, axis=-1, keepdims=True)
    return xc * lax.rsqrt(var + EPS) * g + b


def _silu(x):
    return x * jax.nn.sigmoid(x)


def _ffn_steps(x, g, sc, sh, gt, w1_ref, w2_ref, emit):
    h = (_rms(x, g) * (1.0 + sc) + sh).astype(_BF16)
    acc = None
    for c in range(D_FF // FF_CHUNK):
        yield
        cols = slice(c * FF_CHUNK, (c + 1) * FF_CHUNK)
        hid = jnp.maximum(_dot(h, w1_ref[:, cols]), 0.0)
        part = _dot((hid * hid).astype(_BF16), w2_ref[cols, :])
        acc = part if acc is None else acc + part
    emit(x + gt * acc)


def _interleave(*steps):
    live = list(steps)
    while live:
        for g in list(live):
            try:
                next(g)
            except StopIteration:
                live.remove(g)


def _ffn(x, g, sc, sh, gt, w1_ref, w2_ref):
    out = []
    _interleave(_ffn_steps(x, g, sc, sh, gt, w1_ref, w2_ref, out.append))
    return out[0]


def _adaln_kernel(c_ref, w_ref, b_ref, o_ref):
    s = _silu(c_ref[...]).astype(_BF16)
    o_ref[...] = _dot(s, w_ref[...].astype(_BF16)) + b_ref[...]


def _adaln(c_all, w_ada, b_ada):
    n_rows = c_all.shape[0]
    n_out = N_MOD * D_MODEL
    return pl.pallas_call(
        _adaln_kernel,
        grid=(DEPTH, n_out // ADA_NB),
        in_specs=[
            pl.BlockSpec((n_rows, D_MODEL), lambda l, n: (0, 0)),
            pl.BlockSpec((None, D_MODEL, ADA_NB), lambda l, n: (l, 0, n)),
            pl.BlockSpec((None, 1, ADA_NB), lambda l, n: (l, 0, n)),
        ],
        out_specs=pl.BlockSpec((None, n_rows, ADA_NB), lambda l, n: (l, 0, n)),
        out_shape=jax.ShapeDtypeStruct((DEPTH, n_rows, n_out), _F32),
        compiler_params=pltpu.CompilerParams(
            dimension_semantics=("arbitrary", "arbitrary"), vmem_limit_bytes=VMEM_LIMIT),
        name="adaln",
    )(c_all, w_ada, b_ada.reshape(DEPTH, 1, n_out))


def _conv_block(aext_ref, wdw_ref, base, rows, lanes):
    first = A_HIST - (CONV_W - 1)
    acc = None
    for r in range(SUBLANES):
        n = rows if r == 0 else rows + SUBLANES
        part = None
        for m in range(r, A_HIST + 1, SUBLANES):
            if m < first:
                continue
            lo = base + m - r
            term = aext_ref[lo:lo + n, lanes] * wdw_ref[m - first:m - first + 1, lanes]
            part = term if part is None else part + term
        part = part if r == 0 else part[r:r + rows]
        acc = part if acc is None else acc + part
    return acc


def _even_prompt_kernel(x_ref, mod_ref, gmix_ref, gffn_ref, gfin_ref, win_ref, wdw_ref, bdw_ref,
                        gcln_ref, bcln_ref, wpool_ref, spool_ref, wout_ref, w1_ref, w2_ref,
                        xo_ref, nconv_ref, npool_ref, aext_ref, pext_ref, cat_ref, *, final):
    t = pl.program_id(1)
    n_t = pl.num_programs(1)

    @pl.when(t == 0)
    def _():
        aext_ref[0:A_HIST, :] = jnp.zeros((A_HIST, D_A), _F32)
        pext_ref[0:P_HIST, :] = jnp.zeros((P_HIST, D_B), _F32)

    sh1, sc1, gt1, sh2, sc2, gt2 = _split_mod(mod_ref[...])

    n_sub = TM // SUB
    x_mid = [None] * n_sub

    def mixer(s):
        r0 = s * SUB
        x = x_ref[r0:r0 + SUB, :]
        h = (_rms(x, gmix_ref[...]) * (1.0 + sc1) + sh1).astype(_BF16)
        z = _dot(h, win_ref[...])
        aext_ref[A_HIST + r0:A_HIST + r0 + SUB, :] = z[:, :D_A] * jax.nn.sigmoid(z[:, D_A:2 * D_A])
        pext_ref[P_HIST + r0:P_HIST + r0 + SUB, :] = z[:, 2 * D_A:]

        for c in range(SUB // CONV_ROWS):
            yield
            base = r0 + c * CONV_ROWS
            conv = jnp.concatenate(
                [_conv_block(aext_ref, wdw_ref, base, CONV_ROWS, slice(lb * 128, (lb + 1) * 128))
                 for lb in range(D_A // 128)], axis=1) + bdw_ref[...]
            a_out = _silu(_layernorm(conv, gcln_ref[...], bcln_ref[...]))
            cat_ref[base:base + CONV_ROWS, 0:D_A] = a_out.astype(_BF16)

        yield
        pos = t * TM + r0 + lax.broadcasted_iota(jnp.int32, (SUB, POOL_G), 0) + 1
        for gi, w in enumerate(POOL_WINDOWS):
            lanes = slice(gi * POOL_G, (gi + 1) * POOL_G)
            p_ext = pext_ref[r0:r0 + P_HIST + SUB, lanes]
            win = p_ext
            step = 1
            while step < w:
                win = win + pltpu.roll(win, step, axis=0)
                step *= 2
            cnt = jnp.minimum(w, pos).astype(_F32)
            d = (win[P_HIST:] / cnt - p_ext[P_HIST:]).astype(_BF16)
            b_out = _dot(d, wpool_ref[gi]) * spool_ref[:, lanes]
            cat_ref[r0:r0 + SUB, D_A + gi * POOL_G:D_A + (gi + 1) * POOL_G] = b_out.astype(_BF16)
        x_mid[s] = x + gt1 * _dot(cat_ref[r0:r0 + SUB, :], wout_ref[...])

    def emit(s, x):
        xo_ref[s * SUB:(s + 1) * SUB, :] = _rms(x, gfin_ref[...]) if final else x

    def ffn(s):
        return _ffn_steps(x_mid[s], gffn_ref[...], sc2, sh2, gt2, w1_ref, w2_ref, functools.partial(emit, s))

    _interleave(mixer(0))
    for s in range(1, n_sub):
        _interleave(mixer(s), ffn(s - 1))
    _interleave(ffn(n_sub - 1))

    @pl.when(t == n_t - 1)
    def _():
        nconv_ref[...] = aext_ref[A_HIST + TM - (CONV_W - 1):A_HIST + TM, :]
        npool_ref[...] = pext_ref[P_HIST + TM - (MAX_WIN - 1):P_HIST + TM, :]

    aext_ref[0:A_HIST, :] = aext_ref[TM:TM + A_HIST, :]
    pext_ref[0:P_HIST, :] = pext_ref[TM:TM + P_HIST, :]


def _odd_prompt_kernel(x_ref, mod_ref, gmix_ref, gffn_ref, gfin_ref, win_ref, gv_ref, bv_ref,
                       ws_ref, bs_ref, wout_ref, w1_ref, w2_ref, xo_ref, y_ref, *, final):
    x = x_ref[...]
    sh1, sc1, gt1, sh2, sc2, gt2 = _split_mod(mod_ref[...])
    h = (_rms(x, gmix_ref[...]) * (1.0 + sc1) + sh1).astype(_BF16)
    z = _dot(h, win_ref[...])
    u = z[:, :D_C]
    v = _layernorm(z[:, D_C:], gv_ref[...], bv_ref[...]).astype(_BF16)

    row = lax.broadcasted_iota(jnp.int32, (CHUNK, CHUNK), 0)
    col = lax.broadcasted_iota(jnp.int32, (CHUNK, CHUNK), 1)
    for hh in range(H_C):
        lanes = slice(hh * DH_C, (hh + 1) * DH_C)
        ws = jnp.where(row >= col, ws_ref[hh], 0.0).astype(_BF16)
        for c in range(TM // CHUNK):
            rows = slice(c * CHUNK, (c + 1) * CHUNK)
            s = _dot(ws, v[rows, lanes]) + bs_ref[:, lanes]
            y_ref[rows, lanes] = (u[rows, lanes] * s).astype(_BF16)

    x = x + gt1 * _dot(y_ref[...], wout_ref[...])
    x = _ffn(x, gffn_ref[...], sc2, sh2, gt2, w1_ref, w2_ref)
    if final:
        x = _rms(x, gfin_ref[...])
    xo_ref[...] = x


def _vmem_spec():
    return pl.BlockSpec(memory_space=pltpu.VMEM)


def _prompt_layer(x, mod_l, small, weights, *, even, final):
    batch, seq, _ = x.shape
    n_t = seq // TM
    x_spec = pl.BlockSpec((None, TM, D_MODEL), lambda b, t: (b, t, 0))
    mod_spec = pl.BlockSpec((None, 1, N_MOD * D_MODEL), lambda b, t: (b, 0, 0))
    in_specs = [x_spec, mod_spec] + [_vmem_spec() for _ in range(len(small) + len(weights))]
    params = pltpu.CompilerParams(
        dimension_semantics=("arbitrary", "arbitrary"), vmem_limit_bytes=VMEM_LIMIT)
    if even:
        return pl.pallas_call(
            functools.partial(_even_prompt_kernel, final=final),
            grid=(batch, n_t),
            in_specs=in_specs,
            out_specs=[
                x_spec,
                pl.BlockSpec((None, CONV_W - 1, D_A), lambda b, t: (b, 0, 0)),
                pl.BlockSpec((None, MAX_WIN - 1, D_B), lambda b, t: (b, 0, 0)),
            ],
            out_shape=[
                jax.ShapeDtypeStruct(x.shape, _F32),
                jax.ShapeDtypeStruct((batch, CONV_W - 1, D_A), _F32),
                jax.ShapeDtypeStruct((batch, MAX_WIN - 1, D_B), _F32),
            ],
            scratch_shapes=[
                pltpu.VMEM((A_HIST + TM, D_A), _F32),
                pltpu.VMEM((P_HIST + TM, D_B), _F32),
                pltpu.VMEM((TM, D_A + D_B), _BF16),
            ],
            compiler_params=params,
            name="prompt_even",
        )(x, mod_l, *small, *weights)
    return pl.pallas_call(
        functools.partial(_odd_prompt_kernel, final=final),
        grid=(batch, n_t),
        in_specs=in_specs,
        out_specs=x_spec,
        out_shape=jax.ShapeDtypeStruct(x.shape, _F32),
        scratch_shapes=[pltpu.VMEM((TM, D_C), _BF16)],
        compiler_params=params,
        name="prompt_odd",
    )(x, mod_l, *small, *weights)


def _tile_rows(m, reps):
    return jnp.concatenate([m] * reps, axis=0)


def _even_sample_kernel(x_ref, mod_ref, gmix_ref, gffn_ref, gfin_ref, chist_ref, phist_ref, win_ref,
                        wdw_ref, bdw_ref, gcln_ref, bcln_ref, wpool_ref, spool_ref, wout_ref,
                        w1_ref, w2_ref, xo_ref, anew_ref, pnew_ref, cat_ref, *, n_pos, n_seq, final):
    x = x_ref[...]
    sh1, sc1, gt1, sh2, sc2, gt2 = [_tile_rows(m, n_pos) for m in _split_mod(mod_ref[...])]
    h = (_rms(x, gmix_ref[...]) * (1.0 + sc1) + sh1).astype(_BF16)
    z = _dot(h, win_ref[...])
    anew_ref[...] = z[:, :D_A] * jax.nn.sigmoid(z[:, D_A:2 * D_A])
    pnew_ref[...] = z[:, 2 * D_A:]

    def a_slab(j):
        if j < CONV_W - 1:
            return chist_ref[j]
        return anew_ref[(j - (CONV_W - 1)) * n_seq:(j - (CONV_W - 2)) * n_seq, :]

    def p_slab(j, lanes):
        if j < MAX_WIN - 1:
            return phist_ref[j, :, lanes]
        return pnew_ref[(j - (MAX_WIN - 1)) * n_seq:(j - (MAX_WIN - 2)) * n_seq, lanes]

    for t in range(n_pos):
        rows = slice(t * n_seq, (t + 1) * n_seq)
        acc = jnp.broadcast_to(bdw_ref[...], (n_seq, D_A))
        for k in range(CONV_W):
            acc = acc + a_slab(t + k) * wdw_ref[k:k + 1, :]
        a_out = _silu(_layernorm(acc, gcln_ref[...], bcln_ref[...]))
        cat_ref[rows, 0:D_A] = a_out.astype(_BF16)
        for gi, w in enumerate(POOL_WINDOWS):
            lanes = slice(gi * POOL_G, (gi + 1) * POOL_G)
            p_g = p_slab(MAX_WIN - 1 + t, lanes)
            win = p_g
            for j in range(1, w):
                win = win + p_slab(MAX_WIN - 1 + t - j, lanes)
            cnt = float(min(w, PAST_LEN + t + 1))
            d = (win / cnt - p_g).astype(_BF16)
            b_out = _dot(d, wpool_ref[gi]) * spool_ref[:, lanes]
            cat_ref[rows, D_A + gi * POOL_G:D_A + (gi + 1) * POOL_G] = b_out.astype(_BF16)

    x = x + gt1 * _dot(cat_ref[...], wout_ref[...])
    x = _ffn(x, gffn_ref[...], sc2, sh2, gt2, w1_ref, w2_ref)
    if final:
        x = _rms(x, gfin_ref[...])
    xo_ref[...] = x


def _odd_sample_kernel(x_ref, mod_ref, gmix_ref, gffn_ref, gfin_ref, win_ref, gv_ref, bv_ref,
                       wsv_ref, bs_ref, wout_ref, w1_ref, w2_ref, xo_ref, v_ref, y_ref,
                       *, n_pos, n_seq, final):
    x = x_ref[...]
    sh1, sc1, gt1, sh2, sc2, gt2 = [_tile_rows(m, n_pos) for m in _split_mod(mod_ref[...])]
    h = (_rms(x, gmix_ref[...]) * (1.0 + sc1) + sh1).astype(_BF16)
    z = _dot(h, win_ref[...])
    u = z[:, :D_C]
    v_ref[...] = _layernorm(z[:, D_C:], gv_ref[...], bv_ref[...])
    for t in range(n_pos):
        rows = slice(t * n_seq, (t + 1) * n_seq)
        s = jnp.broadcast_to(bs_ref[t:t + 1, :], (n_seq, D_C))
        for src in range(t + 1):
            s = s + wsv_ref[t * n_pos + src:t * n_pos + src + 1, :] * v_ref[src * n_seq:(src + 1) * n_seq, :]
        y_ref[rows, :] = (u[rows, :] * s).astype(_BF16)

    x = x + gt1 * _dot(y_ref[...], wout_ref[...])
    x = _ffn(x, gffn_ref[...], sc2, sh2, gt2, w1_ref, w2_ref)
    if final:
        x = _rms(x, gfin_ref[...])
    xo_ref[...] = x


def _sample_layer(x, mod_l, small, weights, *, n_pos, n_seq, even, final):
    n_in = 2 + len(small) + len(weights)
    params = pltpu.CompilerParams(vmem_limit_bytes=VMEM_LIMIT)
    rows = n_pos * n_seq
    if even:
        return pl.pallas_call(
            functools.partial(_even_sample_kernel, n_pos=n_pos, n_seq=n_seq, final=final),
            in_specs=[_vmem_spec() for _ in range(n_in)],
            out_specs=[_vmem_spec() for _ in range(3)],
            out_shape=[
                jax.ShapeDtypeStruct((rows, D_MODEL), _F32),
                jax.ShapeDtypeStruct((rows, D_A), _F32),
                jax.ShapeDtypeStruct((rows, D_B), _F32),
            ],
            scratch_shapes=[pltpu.VMEM((rows, D_A + D_B), _BF16)],
            compiler_params=params,
            name="sample_even",
        )(x, mod_l, *small, *weights)
    return pl.pallas_call(
        functools.partial(_odd_sample_kernel, n_pos=n_pos, n_seq=n_seq, final=final),
        in_specs=[_vmem_spec() for _ in range(n_in)],
        out_specs=[_vmem_spec() for _ in range(2)],
        out_shape=[
            jax.ShapeDtypeStruct((rows, D_MODEL), _F32),
            jax.ShapeDtypeStruct((rows, D_C), _F32),
        ],
        scratch_shapes=[pltpu.VMEM((rows, D_C), _BF16)],
        compiler_params=params,
        name="sample_odd",
    )(x, mod_l, *small, *weights)


def kernel(x_prompt, x_sample, c_prompt, c_sample, state_conv, state_pool, w_ada, b_ada, g_mix, g_ffn,
           w_in_ab, w_dw, b_dw, g_conv_ln, b_conv_ln, w_pool, s_pool, w_out_ab, w_in_c, g_v_ln, b_v_ln,
           w_spatial, b_spatial, w_out_c, w_ffn1, w_ffn2, g_final):
    n_batch = x_prompt.shape[0]
    n_seq, n_pos, _ = x_sample.shape
    assert x_prompt.shape[1] % TM == 0 and TM % CHUNK == 0 and TM % SUB == 0 and SUB % CONV_ROWS == 0
    assert all(w & (w - 1) == 0 for w in POOL_WINDOWS) and A_HIST % SUBLANES == 0
    assert n_pos <= CHUNK and n_seq % 8 == 0

    row = lambda a: a.reshape(1, -1)
    bf = lambda a: a.astype(_BF16)

    mod = _adaln(jnp.concatenate([c_prompt, c_sample], axis=0), w_ada, b_ada)
    mod_p = mod[:, :n_batch].reshape(DEPTH, n_batch, 1, N_MOD * D_MODEL)
    mod_s = mod[:, n_batch:]

    xp = x_prompt
    xs = x_sample.transpose(1, 0, 2).reshape(n_pos * n_seq, D_MODEL)
    conv_hist = state_conv.transpose(0, 2, 1, 3)
    pool_hist = state_pool.transpose(0, 2, 1, 3)

    conv_p, pool_p, conv_s, pool_s, v_s = [], [], [], [], []
    for l in range(DEPTH):
        final = l == DEPTH - 1
        common = [row(g_mix[l]), row(g_ffn[l]), row(g_final)]
        ffn_w = [bf(w_ffn1[l]), bf(w_ffn2[l])]
        if l % 2 == 0:
            i = l // 2
            small = common
            weights = [bf(w_in_ab[i]), w_dw[i], row(b_dw[i]), row(g_conv_ln[i]), row(b_conv_ln[i]),
                       bf(w_pool[i]), row(s_pool[i]), bf(w_out_ab[i])] + ffn_w
            xp, cp, pp = _prompt_layer(xp, mod_p[l], small, weights, even=True, final=final)
            conv_p.append(cp)
            pool_p.append(pp)
            xs, a_new, p_new = _sample_layer(
                xs, mod_s[l], small + [conv_hist[i], pool_hist[i]], weights,
                n_pos=n_pos, n_seq=n_seq, even=True, final=final)
            unflat = lambda a: a.reshape(n_pos, n_seq, -1).transpose(1, 0, 2)
            conv_s.append(jnp.concatenate([state_conv[i], unflat(a_new)], axis=1)[:, -(CONV_W - 1):])
            pool_s.append(jnp.concatenate([state_pool[i], unflat(p_new)], axis=1)[:, -(MAX_WIN - 1):])
        else:
            j = l // 2
            bias = jnp.repeat(b_spatial[j].T, DH_C, axis=1)
            head = [bf(w_in_c[j]), row(g_v_ln[j]), row(b_v_ln[j])]
            tail = [bias, bf(w_out_c[j])] + ffn_w
            xp = _prompt_layer(xp, mod_p[l], common, head + [w_spatial[j]] + tail, even=False, final=final)
            ws_small = w_spatial[j][:, :n_pos, :n_pos].transpose(1, 2, 0).reshape(n_pos * n_pos, H_C)
            ws_vec = jnp.repeat(ws_small, DH_C, axis=1)
            xs, v_new = _sample_layer(xs, mod_s[l], common, head + [ws_vec] + tail,
                                      n_pos=n_pos, n_seq=n_seq, even=False, final=final)
            v_s.append(v_new.reshape(n_pos, n_seq, D_C).transpose(1, 0, 2))

    y_sample = xs.reshape(n_pos, n_seq, D_MODEL).transpose(1, 0, 2)
    return (xp, y_sample, jnp.stack(conv_p), jnp.stack(pool_p),
            jnp.stack(conv_s), jnp.stack(pool_s), jnp.stack(v_s))
```
